```python
import jax, jax.numpy as jnp
from jax import lax
import numpy as np

D_MODEL = 1024
BATCH = 4
SEQ = 8192
DEPTH = 1
DEC_BATCH = 8
DEC_SEQ = 4096
PAST_LEN = 128

RWKV_HEAD = 64
RWKV_WIDTH = D_MODEL // 2
RWKV_HEADS = RWKV_WIDTH // RWKV_HEAD
DECAY_LORA = 64
ICLR_LORA = 64
GATE_LORA = 128
GN_EPS = 64e-5
MLA_HEADS = 8
QK_NOPE = 64
QK_ROPE = 32
V_HEAD = 64
Q_LORA = 768
KV_LORA = 256
ROPE_THETA = 10000.0
Q_BLOCK = 128
N_GROUPS = 4
EXPERTS_PER_GROUP = 8
N_EXPERTS = N_GROUPS * EXPERTS_PER_GROUP
TOP_K = 2
D_EXPERT = 256
MOE_BLOCK = 128
N_BRANCHES = 2
EPS = 1e-6

RWKV_COLS = 3 * RWKV_WIDTH + 2 * DECAY_LORA + 2 * ICLR_LORA + GATE_LORA
MLA_COLS = Q_LORA + KV_LORA + QK_ROPE
GATE_COLS = N_BRANCHES * D_MODEL
IN_COLS = RWKV_COLS + MLA_COLS + GATE_COLS
RWKV_SPLITS = (RWKV_WIDTH, 2 * RWKV_WIDTH, 3 * RWKV_WIDTH,
               3 * RWKV_WIDTH + DECAY_LORA, 3 * RWKV_WIDTH + 2 * DECAY_LORA,
               3 * RWKV_WIDTH + 2 * DECAY_LORA + ICLR_LORA,
               3 * RWKV_WIDTH + 2 * DECAY_LORA + 2 * ICLR_LORA)

kernel_name = 'hybrid_rwkv7_mla_hmoe_encoder'


def rms_norm(x, gain):
    xf = x.astype(jnp.float32)
    y = xf * lax.rsqrt(jnp.mean(xf * xf, axis=-1, keepdims=True) + EPS)
    return (y * gain.astype(jnp.float32)).astype(x.dtype)


def centred_shift(p, mu_prev, mu_next):
    prev = jnp.pad(p, ((0, 0), (1, 0), (0, 0)))[:, :-1]
    nxt = jnp.pad(p, ((0, 0), (0, 1), (0, 0)))[:, 1:]
    return p + mu_prev * (prev - p) + mu_next * (nxt - p)


def wkv7_scan(r, w, k, v, kk, a, reverse):
    B, S, H, N = r.shape

    def step(state, inp):
        r_t, w_t, k_t, v_t, kk_t, a_t = inp
        sa = jnp.einsum('bhvk,bhk->bhv', state, -kk_t)
        state = (state * w_t[:, :, None, :] + sa[..., None] * (kk_t * a_t)[:, :, None, :]
                 + v_t[..., None] * k_t[:, :, None, :])
        return state, jnp.einsum('bhvk,bhk->bhv', state, r_t)

    xs = tuple(jnp.moveaxis(t, 1, 0) for t in (r, w, k, v, kk, a))
    _, ys = lax.scan(step, jnp.zeros((B, H, N, N), jnp.float32), xs, reverse=reverse)
    return jnp.moveaxis(ys, 0, 1)


def rwkv7_mixer(p, shift_mu, decay_w0, decay_up, iclr_a0, iclr_up, gate_up, k_k, k_a, r_k, gn_w, gn_b, w_o_rwkv):
    B, S, _ = p.shape
    p = centred_shift(p.astype(jnp.float32), shift_mu[0], shift_mu[1])
    r, k, v, wd_f, wd_b, ad_f, ad_b, gd = jnp.split(p, RWKV_SPLITS, axis=-1)
    hs = (B, S, RWKV_HEADS, RWKV_HEAD)
    kk = (k * k_k).reshape(hs)
    kk = kk / jnp.maximum(jnp.sqrt(jnp.sum(kk * kk, axis=-1, keepdims=True)), 1e-12)
    r_h, v_h = r.reshape(hs), v.reshape(hs)
    ys = []
    for d, (wd, ad) in enumerate(((wd_f, ad_f), (wd_b, ad_b))):
        w_log = -jax.nn.softplus(-(decay_w0[d] + jnp.tanh(wd) @ decay_up[d])) - 0.5
        decay = jnp.exp(-jnp.exp(w_log))
        a = jax.nn.sigmoid(iclr_a0[d] + ad @ iclr_up[d])
        k_d = k * (1.0 + (a - 1.0) * k_a)
        ys.append(wkv7_scan(r_h, decay.reshape(hs), k_d.reshape(hs), v_h, kk, a.reshape(hs), reverse=(d == 1)))
    y = ys[0] + ys[1]
    mu = jnp.mean(y, axis=-1, keepdims=True)
    var = jnp.mean(jnp.square(y - mu), axis=-1, keepdims=True)
    y = ((y - mu) * lax.rsqrt(var + GN_EPS)).reshape(B, S, RWKV_WIDTH) * gn_w + gn_b
    bonus = jnp.sum(r_h * k.reshape(hs) * r_k, axis=-1, keepdims=True) * v_h
    g = jax.nn.sigmoid(gd) @ gate_up
    out = (y + bonus.reshape(B, S, RWKV_WIDTH)) * g
    return out.astype(w_o_rwkv.dtype) @ w_o_rwkv


def rope_tables(S):
    inv = 1.0 / (ROPE_THETA ** (jnp.arange(0, QK_ROPE, 2, dtype=jnp.float32) / QK_ROPE))
    ang = jnp.arange(S, dtype=jnp.float32)[:, None] * inv[None, :]
    return jnp.cos(ang), jnp.sin(ang)


def apply_rope(x, cos, sin):
    xf = x.astype(jnp.float32)
    x1, x2 = xf[..., :QK_ROPE // 2], xf[..., QK_ROPE // 2:]
    return jnp.concatenate([x1 * cos - x2 * sin, x1 * sin + x2 * cos], axis=-1).astype(x.dtype)


def mla_mixer(p, q_a_norm, w_uq, kv_a_norm, w_uk, w_uv, q_nope_norm, q_rope_norm, k_nope_norm, k_rope_norm, w_o_mla):
    B, S, _ = p.shape
    c_q, c_kv, k_pe = jnp.split(p, [Q_LORA, Q_LORA + KV_LORA], axis=-1)
    q = (rms_norm(c_q, q_a_norm) @ w_uq).reshape(B, S, MLA_HEADS, QK_NOPE + QK_ROPE)
    q_nope, q_pe = jnp.split(q, [QK_NOPE], axis=-1)
    c_kv = rms_norm(c_kv, kv_a_norm)
    k_nope = (c_kv @ w_uk).reshape(B, S, MLA_HEADS, QK_NOPE)
    v = (c_kv @ w_uv).reshape(B, S, MLA_HEADS, V_HEAD)
    cos, sin = rope_tables(S)
    q_nope = rms_norm(q_nope, q_nope_norm)
    k_nope = rms_norm(k_nope, k_nope_norm)
    q_pe = apply_rope(rms_norm(q_pe, q_rope_norm), cos[:, None, :], sin[:, None, :])
    k_pe = apply_rope(rms_norm(k_pe, k_rope_norm), cos, sin)
    scale = (QK_NOPE + QK_ROPE) ** -0.5
    n_blk = S // Q_BLOCK

    def to_blocks(t):
        return jnp.moveaxis(t.reshape(B, n_blk, Q_BLOCK, *t.shape[2:]), 1, 0)

    def attend(blk):
        qn, qr = blk
        s = (jnp.einsum('bqhd,bkhd->bhqk', qn, k_nope, preferred_element_type=jnp.float32)
             + jnp.einsum('bqhr,bkr->bhqk', qr, k_pe, preferred_element_type=jnp.float32)) * scale
        pr = jax.nn.softmax(s, axis=-1).astype(v.dtype)
        return jnp.einsum('bhqk,bkhd->bqhd', pr, v)

    o = lax.map(attend, (to_blocks(q_nope), to_blocks(q_pe)))
    o = jnp.moveaxis(o, 0, 1).reshape(B, S, MLA_HEADS * V_HEAD)
    return o @ w_o_mla


def hier_moe(h, w_rg, b_rg, w_re, b_re, w13, w2):
    B, S, D = h.shape
    N = B * S
    xt = h.reshape(N, D)
    g_prob = jax.nn.softmax((xt @ w_rg).astype(jnp.float32) + b_rg.astype(jnp.float32), axis=-1)
    grp = jnp.argmax(g_prob, axis=-1).astype(jnp.int32)
    p_grp = jnp.max(g_prob, axis=-1)
    e_logits = ((xt @ w_re).astype(jnp.float32) + b_re.astype(jnp.float32)).reshape(N, N_GROUPS, EXPERTS_PER_GROUP)
    e_in = jnp.take_along_axis(e_logits, grp[:, None, None], axis=1)[:, 0]
    top_p, top_i = lax.top_k(jax.nn.softmax(e_in, axis=-1), TOP_K)
    top_p = top_p / jnp.sum(top_p, axis=-1, keepdims=True)
    weight = (p_grp[:, None] * top_p).reshape(-1)
    expert = (grp[:, None] * EXPERTS_PER_GROUP + top_i.astype(jnp.int32)).reshape(-1)
    token = jnp.repeat(jnp.arange(N, dtype=jnp.int32), TOP_K)
    A = N * TOP_K
    order = jnp.argsort(expert)
    e_sorted = expert[order]
    counts = jnp.bincount(expert, length=N_EXPERTS)
    padded = (counts + MOE_BLOCK - 1) // MOE_BLOCK * MOE_BLOCK
    pad_end = jnp.cumsum(padded)
    pad_start = pad_end - padded
    start = jnp.cumsum(counts) - counts
    dest = pad_start[e_sorted] + jnp.arange(A, dtype=jnp.int32) - start[e_sorted]
    P = A + N_EXPERTS * MOE_BLOCK
    n_blk = P // MOE_BLOCK
    buf_tok = jnp.full((P,), N, jnp.int32).at[dest].set(token[order])
    buf_w = jnp.zeros((P,), jnp.float32).at[dest].set(weight[order])
    blk_e = jnp.minimum(jnp.searchsorted(pad_end, jnp.arange(n_blk, dtype=jnp.int32) * MOE_BLOCK, side='right'),
                        N_EXPERTS - 1)
    x_pad = jnp.concatenate([xt, jnp.zeros((1, D), xt.dtype)], axis=0)
    xb = x_pad[buf_tok].reshape(n_blk, MOE_BLOCK, D)

    def expert_block(args):
        xblk, e = args
        gt, up = jnp.split(xblk @ w13[e], 2, axis=-1)
        return (jax.nn.silu(gt) * up) @ w2[e]

    yb = lax.map(expert_block, (xb, blk_e)).reshape(P, D)
    out = jax.ops.segment_sum(yb.astype(jnp.float32) * buf_w[:, None], buf_tok, num_segments=N + 1)[:N]
    return out.astype(h.dtype).reshape(B, S, D)


def encoder_layer(x, c, w_ada, b_ada, norm1, w_in, shift_mu, decay_w0, decay_up, iclr_a0, iclr_up, gate_up,
                  k_k, k_a, r_k, gn_w, gn_b, w_o_rwkv, q_a_norm, w_uq, kv_a_norm, w_uk, w_uv,
                  q_nope_norm, q_rope_norm, k_nope_norm, k_rope_norm, w_o_mla, b_merge, w_out,
                  norm2, w_rg, b_rg, w_re, b_re, w13, w2):
    mod = (jax.nn.silu(c) @ w_ada + b_ada)[:, None, :]
    shift1, scale1, gate1, shift2, scale2, gate2 = jnp.split(mod, 6, axis=-1)
    h = rms_norm(x, norm1) * (1 + scale1) + shift1
    p = h @ w_in
    p_rwkv, p_mla, p_gate = jnp.split(p, [RWKV_COLS, RWKV_COLS + MLA_COLS], axis=-1)
    y_a = rwkv7_mixer(p_rwkv, shift_mu, decay_w0, decay_up, iclr_a0, iclr_up, gate_up, k_k, k_a, r_k,
                      gn_w, gn_b, w_o_rwkv).astype(x.dtype)
    y_b = mla_mixer(p_mla, q_a_norm, w_uq, kv_a_norm, w_uk, w_uv, q_nope_norm, q_rope_norm,
                    k_nope_norm, k_rope_norm, w_o_mla).astype(x.dtype)
    gates = jax.nn.sigmoid((p_gate + b_merge).astype(jnp.float32)).astype(x.dtype)
    g_a, g_b = jnp.split(gates, N_BRANCHES, axis=-1)
    x = x + gate1 * ((g_a * y_a + g_b * y_b) @ w_out)
    h2 = rms_norm(x, norm2) * (1 + scale2) + shift2
    x = x + gate2 * hier_moe(h2, w_rg, b_rg, w_re, b_re, w13, w2)
    return x


def setup_inputs(seed: int = 0) -> dict:
    key = jax.random.key(seed)
    ks = iter(list(jax.random.split(key, 64)))

    def nrm(shape, std):
        return std * jax.random.normal(next(ks), shape, jnp.float32)

    def gain(shape):
        return 1.0 + 0.05 * jax.random.normal(next(ks), shape, jnp.float32)

    L, D, C = DEPTH, D_MODEL, RWKV_WIDTH
    return {
        'x_prompt': nrm((BATCH, SEQ, D), 1.0),
        'x_sample': nrm((DEC_BATCH, DEC_SEQ, D), 1.0),
        'c_prompt': nrm((BATCH, D), 1.0),
        'c_sample': nrm((DEC_BATCH, D), 1.0),
        'w_ada': nrm((L, D, 6 * D), 0.5 * D ** -0.5),
        'b_ada': nrm((L, 6 * D), 0.02),
        'norm1': gain((L, D)),
        'w_in': nrm((L, D, IN_COLS), D ** -0.5),
        'shift_mu': 0.5 * jax.random.uniform(next(ks), (L, 2, RWKV_COLS), jnp.float32),
        'decay_w0': jax.random.uniform(next(ks), (L, 2, C), jnp.float32, minval=-6.0, maxval=0.0),
        'decay_up': nrm((L, 2, DECAY_LORA, C), 0.1),
        'iclr_a0': nrm((L, 2, C), 0.5),
        'iclr_up': nrm((L, 2, ICLR_LORA, C), 0.5 * ICLR_LORA ** -0.5),
        'gate_up': nrm((L, GATE_LORA, C), GATE_LORA ** -0.5),
        'k_k': 0.85 + 0.05 * jax.random.normal(next(ks), (L, C), jnp.float32),
        'k_a': gain((L, C)),
        'r_k': nrm((L, RWKV_HEADS, RWKV_HEAD), 0.1),
        'gn_w': gain((L, C)),
        'gn_b': nrm((L, C), 0.02),
        'w_o_rwkv': nrm((L, C, D), C ** -0.5),
        'q_a_norm': gain((L, Q_LORA)),
        'w_uq': nrm((L, Q_LORA, MLA_HEADS * (QK_NOPE + QK_ROPE)), Q_LORA ** -0.5),
        'kv_a_norm': gain((L, KV_LORA)),
        'w_uk': nrm((L, KV_LORA, MLA_HEADS * QK_NOPE), KV_LORA ** -0.5),
        'w_uv': nrm((L, KV_LORA, MLA_HEADS * V_HEAD), KV_LORA ** -0.5),
        'q_nope_norm': gain((L, QK_NOPE)),
        'q_rope_norm': gain((L, QK_ROPE)),
        'k_nope_norm': gain((L, QK_NOPE)),
        'k_rope_norm': gain((L, QK_ROPE)),
        'w_o_mla': nrm((L, MLA_HEADS * V_HEAD, D), (MLA_HEADS * V_HEAD) ** -0.5),
        'b_merge': nrm((L, GATE_COLS), 0.1),
        'w_out': nrm((L, D, D), D ** -0.5),
        'norm2': gain((L, D)),
        'w_rg': nrm((L, D, N_GROUPS), D ** -0.5),
        'b_rg': nrm((L, N_GROUPS), 0.01),
        'w_re': nrm((L, D, N_EXPERTS), D ** -0.5),
        'b_re': nrm((L, N_EXPERTS), 0.01),
        'w13': nrm((L, N_EXPERTS, D, 2 * D_EXPERT), D ** -0.5),
        'w2': nrm((L, N_EXPERTS, D_EXPERT, D), D_EXPERT ** -0.5),
    }


def reference(x_prompt, x_sample, c_prompt, c_sample, w_ada, b_ada, norm1, w_in, shift_mu, decay_w0, decay_up,
              iclr_a0, iclr_up, gate_up, k_k, k_a, r_k, gn_w, gn_b, w_o_rwkv, q_a_norm, w_uq, kv_a_norm,
              w_uk, w_uv, q_nope_norm, q_rope_norm, k_nope_norm, k_rope_norm, w_o_mla, b_merge, w_out,
              norm2, w_rg, b_rg, w_re, b_re, w13, w2):
    weights = (w_ada, b_ada, norm1, w_in, shift_mu, decay_w0, decay_up, iclr_a0, iclr_up, gate_up,
               k_k, k_a, r_k, gn_w, gn_b, w_o_rwkv, q_a_norm, w_uq, kv_a_norm, w_uk, w_uv,
               q_nope_norm, q_rope_norm, k_nope_norm, k_rope_norm, w_o_mla, b_merge, w_out,
               norm2, w_rg, b_rg, w_re, b_re, w13, w2)

    def trunk(x, c):
        for l in range(DEPTH):
            x = encoder_layer(x, c, *[w[l] for w in weights])
        return x

    y_prompt = trunk(x_prompt, c_prompt)
    y_sample = trunk(x_sample, c_sample)
    return (y_prompt, y_sample)
```

```python
import functools

import jax
import jax.numpy as jnp
from jax import lax
from jax.experimental import pallas as pl
from jax.experimental.pallas import tpu as pltpu

F32 = jnp.float32
BF16 = jnp.bfloat16

D = 1024
C = 512
HD = 64
N_PAIR = C // 128
RWKV_COLS = 1920
GN_EPS = 64e-5
EPS = 1e-6
MLA_H = 8
QK_NOPE = 64
QK_ROPE = 32
Q_LORA = 768
KV_LORA = 256
MLA_PAD = 1152
ROPE_THETA = 10000.0
N_GROUPS = 4
EPG = 8
N_EXPERTS = 32
D_EXPERT = 256
LANE = 128
CHUNK = 64
MOE_ROWS = 256
TOKEN_TILE = 256
VMEM_LIMIT = 48 * 1024 * 1024


def _cp(sem):
    return pltpu.CompilerParams(dimension_semantics=sem, vmem_limit_bytes=VMEM_LIMIT)


def _dot(a, b):
    return jnp.dot(a.astype(BF16), b.astype(BF16), preferred_element_type=F32)


def _dot_nt(a, b):
    return lax.dot_general(a.astype(BF16), b.astype(BF16), (((1,), (1,)), ((), ())),
                           preferred_element_type=F32)


def _dot_tn(a, b):
    return lax.dot_general(a.astype(BF16), b.astype(BF16), (((0,), (0,)), ((), ())),
                           preferred_element_type=F32)


def _sigmoid(x):
    return 1.0 / (1.0 + jnp.exp(-x))


def _rms(x, gain):
    return x * lax.rsqrt(jnp.mean(x * x, axis=-1, keepdims=True) + EPS) * gain


def _head_sum(x):
    lo = lax.broadcasted_iota(jnp.int32, (1, LANE), 1) < HD
    s0 = jnp.sum(jnp.where(lo, x, 0.0), axis=-1, keepdims=True)
    s1 = jnp.sum(jnp.where(lo, 0.0, x), axis=-1, keepdims=True)
    return jnp.where(lo, s0, s1)


def _ada_body(c_ref, w_ref, b_ref, o_ref):
    c = c_ref[...]
    o_ref[...] = _dot(c * _sigmoid(c), w_ref[...]) + b_ref[...]


def _ada(c_all, w, b):
    rows, n = c_all.shape[0], w.shape[1]
    tn = 1536
    return pl.pallas_call(
        _ada_body, grid=(n // tn,),
        in_specs=[pl.BlockSpec((rows, D), lambda j: (0, 0)),
                  pl.BlockSpec((D, tn), lambda j: (0, j)),
                  pl.BlockSpec((1, tn), lambda j: (0, j))],
        out_specs=pl.BlockSpec((rows, tn), lambda j: (0, j)),
        out_shape=jax.ShapeDtypeStruct((rows, n), F32),
        compiler_params=_cp(("parallel",)), name="ada")(c_all, w, b)


def _inproj_body(x_ref, sc_ref, sh_ref, g_ref, wr_ref, wm_ref, wg_ref, bm_ref,
                 pr_ref, pm_ref, gt_ref):
    h = _rms(x_ref[...], g_ref[...]) * (1.0 + sc_ref[0]) + sh_ref[0]
    hb = h.astype(BF16)
    pr_ref[...] = _dot(hb, wr_ref[...])
    pm_ref[...] = _dot(hb, wm_ref[...])
    gt_ref[...] = _sigmoid(_dot(hb, wg_ref[...]) + bm_ref[...])


def _inproj(x2, scale1, shift1, norm1, w_r, w_m, w_g, b_merge, seq):
    n = x2.shape[0]
    tm = min(TOKEN_TILE, seq)
    bmap = lambda i: ((i * tm) // seq, 0, 0)
    full = lambda i: (0, 0)
    return pl.pallas_call(
        _inproj_body, grid=(n // tm,),
        in_specs=[pl.BlockSpec((tm, D), lambda i: (i, 0)),
                  pl.BlockSpec((1, 1, D), bmap), pl.BlockSpec((1, 1, D), bmap),
                  pl.BlockSpec((1, D), full),
                  pl.BlockSpec((D, RWKV_COLS), full), pl.BlockSpec((D, MLA_PAD), full),
                  pl.BlockSpec((D, 2 * D), full), pl.BlockSpec((1, 2 * D), full)],
        out_specs=[pl.BlockSpec((tm, RWKV_COLS), lambda i: (i, 0)),
                   pl.BlockSpec((tm, MLA_PAD), lambda i: (i, 0)),
                   pl.BlockSpec((tm, 2 * D), lambda i: (i, 0))],
        out_shape=[jax.ShapeDtypeStruct((n, RWKV_COLS), F32),
                   jax.ShapeDtypeStruct((n, MLA_PAD), F32),
                   jax.ShapeDtypeStruct((n, 2 * D), F32)],
        compiler_params=_cp(("parallel",)), name="inproj")(
            x2, scale1, shift1, norm1, w_r, w_m, w_g, b_merge)


def _head_norm_rope(xh, gain, cos, sin_a, sin_b):
    lane = lax.broadcasted_iota(jnp.int32, (1, LANE), 1)
    is_nope = lane < QK_NOPE
    is_rope = jnp.logical_and(lane >= QK_NOPE, lane < QK_NOPE + QK_ROPE)
    sq = xh * xh
    ms_n = jnp.sum(jnp.where(is_nope, sq, 0.0), axis=-1, keepdims=True) * (1.0 / QK_NOPE)
    ms_r = jnp.sum(jnp.where(is_rope, sq, 0.0), axis=-1, keepdims=True) * (1.0 / QK_ROPE)
    inv = jnp.where(is_nope, lax.rsqrt(ms_n + EPS), lax.rsqrt(ms_r + EPS))
    y = xh * inv * gain
    half = QK_ROPE // 2
    return y * cos + pltpu.roll(y, LANE - half, 1) * sin_a + pltpu.roll(y, half, 1) * sin_b


def _mla_prep_body(p_ref, qan_ref, wuq_ref, kvn_ref, wuk_ref, wuv_ref, e_ref, qg_ref, kg_ref,
                   cos_ref, sa_ref, sb_ref, q_ref, k_ref, v_ref):
    cqn = _rms(p_ref[:, :Q_LORA], qan_ref[...])
    ckvn = _rms(p_ref[:, Q_LORA:Q_LORA + KV_LORA], kvn_ref[...])
    kpe = p_ref[:, Q_LORA + KV_LORA:]
    kpe_hi = kpe.astype(BF16)
    kpe_lo = (kpe - kpe_hi.astype(F32)).astype(BF16)
    q = _dot(cqn, wuq_ref[...])
    k = _dot(ckvn, wuk_ref[...]) + _dot(kpe_hi, e_ref[...]) + _dot(kpe_lo, e_ref[...])
    v_ref[...] = _dot(ckvn, wuv_ref[...]).astype(BF16)
    cos, sa, sb = cos_ref[...], sa_ref[...], sb_ref[...]
    scale = (QK_NOPE + QK_ROPE) ** -0.5
    for h in range(MLA_H):
        sl = slice(h * LANE, (h + 1) * LANE)
        q_ref[:, sl] = (_head_norm_rope(q[:, sl], qg_ref[...], cos, sa, sb) * scale).astype(BF16)
        k_ref[:, sl] = _head_norm_rope(k[:, sl], kg_ref[...], cos, sa, sb).astype(BF16)


def _mla_prep(p_mla, q_a_norm, w_uq, kv_a_norm, w_uk, w_uv, e_mat, q_gain, k_gain,
              cos_t, sin_a, sin_b, seq):
    n = p_mla.shape[0]
    tm = min(TOKEN_TILE, seq)
    full = lambda i: (0, 0)
    tab = lambda i: (i % (seq // tm), 0)
    hp = MLA_H * LANE
    return pl.pallas_call(
        _mla_prep_body, grid=(n // tm,),
        in_specs=[pl.BlockSpec((tm, MLA_PAD), lambda i: (i, 0)),
                  pl.BlockSpec((1, Q_LORA), full), pl.BlockSpec((Q_LORA, hp), full),
                  pl.BlockSpec((1, KV_LORA), full), pl.BlockSpec((KV_LORA, hp), full),
                  pl.BlockSpec((KV_LORA, hp), full), pl.BlockSpec((LANE, hp), full),
                  pl.BlockSpec((1, LANE), full), pl.BlockSpec((1, LANE), full),
                  pl.BlockSpec((tm, LANE), tab), pl.BlockSpec((tm, LANE), tab),
                  pl.BlockSpec((tm, LANE), tab)],
        out_specs=[pl.BlockSpec((tm, hp), lambda i: (i, 0))] * 3,
        out_shape=[jax.ShapeDtypeStruct((n, hp), BF16)] * 3,
        compiler_params=_cp(("parallel",)), name="mla_prep")(
            p_mla, q_a_norm, w_uq, kv_a_norm, w_uk, w_uv, e_mat, q_gain, k_gain,
            cos_t, sin_a, sin_b)


def _attn_body(nk, q_ref, k_ref, v_ref, o_ref, m_ref, l_ref, acc_ref):
    j = pl.program_id(3)

    @pl.when(j == 0)
    def _():
        m_ref[...] = jnp.full(m_ref.shape, -jnp.inf, F32)
        l_ref[...] = jnp.zeros(l_ref.shape, F32)
        acc_ref[...] = jnp.zeros(acc_ref.shape, F32)

    s = _dot_nt(q_ref[0], k_ref[0])
    m_old = m_ref[...]
    m_new = jnp.maximum(m_old, jnp.max(s, axis=-1, keepdims=True))
    alpha = jnp.exp(m_old - m_new)
    p = jnp.exp(s - m_new)
    l_ref[...] = alpha * l_ref[...] + jnp.sum(p, axis=-1, keepdims=True)
    acc_ref[...] = alpha * acc_ref[...] + _dot(p, v_ref[0])
    m_ref[...] = m_new

    @pl.when(j == nk - 1)
    def _():
        o_ref[0] = (acc_ref[...] / l_ref[...]).astype(o_ref.dtype)


def _attention(q, k, v):
    b, s, _ = q.shape
    tq = min(512, s)
    tk = min(512, s)
    return pl.pallas_call(
        functools.partial(_attn_body, s // tk), grid=(b, MLA_H, s // tq, s // tk),
        in_specs=[pl.BlockSpec((1, tq, LANE), lambda b_, h, i, j: (b_, i, h)),
                  pl.BlockSpec((1, tk, LANE), lambda b_, h, i, j: (b_, j, h)),
                  pl.BlockSpec((1, tk, LANE), lambda b_, h, i, j: (b_, j, h))],
        out_specs=pl.BlockSpec((1, tq, LANE), lambda b_, h, i, j: (b_, i, h)),
        out_shape=jax.ShapeDtypeStruct(q.shape, BF16),
        scratch_shapes=[pltpu.VMEM((tq, 1), F32), pltpu.VMEM((tq, 1), F32),
                        pltpu.VMEM((tq, LANE), F32)],
        compiler_params=_cp(("parallel", "parallel", "parallel", "arbitrary")),
        name="attention")(q, k, v)


def _rwkv_prep_body(seq, p_ref, hp_ref, hn_ref, mu_ref, w0_ref, dup_ref, a0_ref, iup_ref,
                    gup_ref, kk_ref, ka_ref, rk_ref,
                    r_ref, v_ref, kkn_ref, bonus_ref, g_ref, kd_ref, lw_ref, bb_ref):
    tm = p_ref.shape[0]
    i = pl.program_id(0)
    p = p_ref[...]
    at_start = (i * tm) % seq == 0
    at_end = ((i + 1) * tm) % seq == 0
    prow = jnp.where(at_start, 0.0, hp_ref[7:8, :])
    nrow = jnp.where(at_end, 0.0, hn_ref[0:1, :])
    row = lax.broadcasted_iota(jnp.int32, (tm, 1), 0)
    prev = jnp.where(row == 0, prow, pltpu.roll(p, 1, 0))
    nxt = jnp.where(row == tm - 1, nrow, pltpu.roll(p, tm - 1, 0))
    xs = p + mu_ref[0:1, :] * (prev - p) + mu_ref[1:2, :] * (nxt - p)

    lora_w = _dot(jnp.tanh(xs[:, 3 * C:3 * C + LANE]), dup_ref[...])
    lora_a = _dot(xs[:, 3 * C + LANE:3 * C + 2 * LANE], iup_ref[...])
    g_ref[...] = _dot(_sigmoid(xs[:, 3 * C + 2 * LANE:]), gup_ref[...])

    for j in range(N_PAIR):
        sl = slice(j * LANE, (j + 1) * LANE)
        r = xs[:, sl]
        k = xs[:, C + j * LANE:C + (j + 1) * LANE]
        v = xs[:, 2 * C + j * LANE:2 * C + (j + 1) * LANE]
        kk = k * kk_ref[:, sl]
        kk = kk / jnp.maximum(jnp.sqrt(_head_sum(kk * kk)), 1e-12)
        r_ref[:, sl] = r
        v_ref[:, sl] = v
        kkn_ref[:, sl] = kk
        bonus_ref[:, sl] = _head_sum(r * k * rk_ref[:, sl]) * v
        for d in range(2):
            dsl = slice(d * C + j * LANE, d * C + (j + 1) * LANE)
            z = -(w0_ref[d:d + 1, sl] + lora_w[:, dsl])
            softplus = jnp.maximum(z, 0.0) + jnp.log(1.0 + jnp.exp(-jnp.abs(z)))
            lw_ref[d, :, sl] = -jnp.exp(-softplus - 0.5)
            a = _sigmoid(a0_ref[d:d + 1, sl] + lora_a[:, dsl])
            kd_ref[d, :, sl] = k * (1.0 + (a - 1.0) * ka_ref[:, sl])
            bb_ref[d, :, sl] = kk * a


def _rwkv_prep(p_rwkv, shift_mu, decay_w0, decay_up2, iclr_a0, iclr_up2, gate_up, k_k, k_a, r_k,
               seq):
    n = p_rwkv.shape[0]
    tm = min(TOKEN_TILE, seq)
    full = lambda i: (0, 0)
    nb8 = n // 8
    tok = pl.BlockSpec((tm, C), lambda i: (i, 0))
    dirs = pl.BlockSpec((2, tm, C), lambda i: (0, i, 0))
    return pl.pallas_call(
        functools.partial(_rwkv_prep_body, seq), grid=(n // tm,),
        in_specs=[pl.BlockSpec((tm, RWKV_COLS), lambda i: (i, 0)),
                  pl.BlockSpec((8, RWKV_COLS), lambda i: (jnp.maximum(i * (tm // 8) - 1, 0), 0)),
                  pl.BlockSpec((8, RWKV_COLS),
                               lambda i: (jnp.minimum((i + 1) * (tm // 8), nb8 - 1), 0)),
                  pl.BlockSpec((2, RWKV_COLS), full), pl.BlockSpec((2, C), full),
                  pl.BlockSpec((LANE, 2 * C), full), pl.BlockSpec((2, C), full),
                  pl.BlockSpec((LANE, 2 * C), full), pl.BlockSpec((LANE, C), full),
                  pl.BlockSpec((1, C), full), pl.BlockSpec((1, C), full),
                  pl.BlockSpec((1, C), full)],
        out_specs=[tok, tok, tok, tok, tok, dirs, dirs, dirs],
        out_shape=[jax.ShapeDtypeStruct((n, C), F32)] * 5
                  + [jax.ShapeDtypeStruct((2, n, C), F32)] * 3,
        compiler_params=_cp(("parallel",)), name="rwkv_prep")(
            p_rwkv, p_rwkv, p_rwkv, shift_mu, decay_w0, decay_up2, iclr_a0, iclr_up2, gate_up,
            k_k, k_a, r_k)


def _stack_heads(x):
    lo = lax.broadcasted_iota(jnp.int32, (1, LANE), 1) < HD
    return jnp.concatenate([jnp.where(lo, x, 0.0), jnp.where(lo, 0.0, x)], axis=0)


def _fold_heads(x):
    return x[:CHUNK] + x[CHUNK:]


def _unit_tri_inverse(a, eye):
    row = lax.broadcasted_iota(jnp.int32, a.shape, 0)
    col = lax.broadcasted_iota(jnp.int32, a.shape, 1)
    diag16 = (row // 16) == (col // 16)
    dg = jnp.where(diag16, a, 0.0)
    lo = a - dg
    p = eye + dg
    q = dg
    for _ in range(3):
        q = _dot(q, q)
        p = p + _dot(p, q)
    n1 = _dot(p, lo)
    n2 = _dot(n1, n1)
    in1 = eye + n1
    return _dot(in1 + _dot(in1, n2), p)


def _wkv_chunk(r, v, kk, kd, lw, bb, s_ref, y_ref, d, rev):
    L = CHUNK
    row = lax.broadcasted_iota(jnp.int32, (L, 1), 0)
    cum = lw
    step = 1
    while step < L:
        if rev:
            cum = cum + jnp.where(row < L - step, pltpu.roll(cum, L - step, 0), 0.0)
        else:
            cum = cum + jnp.where(row >= step, pltpu.roll(cum, step, 0), 0.0)
        step *= 2
    tot = cum[0:1] if rev else cum[L - 1:L]
    g_in = jnp.exp(cum)
    g_ex = jnp.exp(cum - lw)
    g_inv = jnp.exp(-cum)
    g_tail = jnp.exp(tot - cum)
    a_t = -kk * g_ex
    b_p = bb * g_inv
    k_p = kd * g_inv
    r_t = r * g_in
    b_h = bb * g_tail
    k_h = kd * g_tail
    g_tot = jnp.exp(tot)

    r2 = lax.broadcasted_iota(jnp.int32, (2 * L, 2 * L), 0)
    c2 = lax.broadcasted_iota(jnp.int32, (2 * L, 2 * L), 1)
    same = (r2 // L) == (c2 // L)
    before = (c2 > r2) if rev else (c2 < r2)
    m_strict = jnp.logical_and(same, before)
    m_incl = jnp.logical_and(same, jnp.logical_or(before, r2 == c2))
    eye = jnp.where(r2 == c2, 1.0, 0.0)

    for j in range(N_PAIR):
        sl = slice(j * LANE, (j + 1) * LANE)
        at_st = _stack_heads(a_t[:, sl])
        rt_st = _stack_heads(r_t[:, sl])
        v_st = _stack_heads(v[:, sl])
        bp2 = jnp.concatenate([b_p[:, sl], b_p[:, sl]], axis=0)
        kp2 = jnp.concatenate([k_p[:, sl], k_p[:, sl]], axis=0)
        gram = _dot_nt(jnp.concatenate([at_st, rt_st], axis=0),
                       jnp.concatenate([bp2, kp2], axis=0))
        a_ab = jnp.where(m_strict, gram[:2 * L, :2 * L], 0.0)
        a_ak = jnp.where(m_strict, gram[:2 * L, 2 * L:], 0.0)
        a_rb = jnp.where(m_incl, gram[2 * L:, :2 * L], 0.0)
        a_rk = jnp.where(m_incl, gram[2 * L:, 2 * L:], 0.0)
        t_inv = _unit_tri_inverse(a_ab, eye)
        x = _dot(t_inv, jnp.concatenate([at_st, a_ak], axis=1))
        p1_st = x[:, :LANE]
        u0_st = _dot(x[:, LANE:], v_st)
        z = _dot(a_rb, jnp.concatenate([p1_st, u0_st], axis=1))
        r2m = r_t[:, sl] + _fold_heads(z[:, :LANE])
        y0 = _fold_heads(z[:, LANE:] + _dot(a_rk, v_st))
        p1 = _fold_heads(p1_st)
        u0 = _fold_heads(u0_st)
        m_mat = eye * g_tot[:, sl] + jnp.where(same, _dot_tn(p1, b_h[:, sl]), 0.0)
        c_mat = jnp.where(same, _dot_tn(jnp.concatenate([u0, v[:, sl]], axis=0),
                                        jnp.concatenate([b_h[:, sl], k_h[:, sl]], axis=0)), 0.0)
        s0 = s_ref[d, j]
        y_ref[0, 0, :, sl] = _dot_nt(r2m, s0) + y0
        s_ref[d, j] = _dot(s0, m_mat) + c_mat


def _rwkv_chunk_body(rf_ref, vf_ref, kkf_ref, kdf_ref, lwf_ref, bbf_ref,
                     rb_ref, vb_ref, kkb_ref, kdb_ref, lwb_ref, bbb_ref,
                     yf_ref, yb_ref, s_ref):
    @pl.when(pl.program_id(1) == 0)
    def _():
        s_ref[...] = jnp.zeros(s_ref.shape, F32)

    _wkv_chunk(rf_ref[0], vf_ref[0], kkf_ref[0], kdf_ref[0, 0], lwf_ref[0, 0], bbf_ref[0, 0],
               s_ref, yf_ref, 0, False)
    _wkv_chunk(rb_ref[0], vb_ref[0], kkb_ref[0], kdb_ref[0, 0], lwb_ref[0, 0], bbb_ref[0, 0],
               s_ref, yb_ref, 1, True)


def _rwkv_chunk(r, v, kk, kd, lw, bb):
    b, s, _ = r.shape
    nc = s // CHUNK
    tok_f = pl.BlockSpec((1, CHUNK, C), lambda b_, c: (b_, c, 0))
    tok_b = pl.BlockSpec((1, CHUNK, C), lambda b_, c: (b_, nc - 1 - c, 0))
    dir_f = pl.BlockSpec((1, 1, CHUNK, C), lambda b_, c: (0, b_, c, 0))
    dir_b = pl.BlockSpec((1, 1, CHUNK, C), lambda b_, c: (1, b_, nc - 1 - c, 0))
    y_shape = jax.ShapeDtypeStruct((1, b, s, C), F32)
    y_f = pl.BlockSpec((1, 1, CHUNK, C), lambda b_, c: (0, b_, c, 0))
    y_b = pl.BlockSpec((1, 1, CHUNK, C), lambda b_, c: (0, b_, nc - 1 - c, 0))
    return pl.pallas_call(
        _rwkv_chunk_body, grid=(b, nc),
        in_specs=[tok_f, tok_f, tok_f, dir_f, dir_f, dir_f,
                  tok_b, tok_b, tok_b, dir_b, dir_b, dir_b],
        out_specs=[y_f, y_b], out_shape=[y_shape, y_shape],
        scratch_shapes=[pltpu.VMEM((2, N_PAIR, LANE, LANE), F32)],
        compiler_params=_cp(("parallel", "arbitrary")), name="rwkv_chunk")(
            r, v, kk, kd, lw, bb, r, v, kk, kd, lw, bb)


def _route(logits):
    lane = lax.broadcasted_iota(jnp.int32, (1, LANE), 1)
    big = jnp.int32(LANE)
    in_g = lane < N_GROUPS
    gl = jnp.where(in_g, logits, -jnp.inf)
    ge = jnp.where(in_g, jnp.exp(gl - jnp.max(gl, axis=-1, keepdims=True)), 0.0)
    gp = ge / jnp.sum(ge, axis=-1, keepdims=True)
    p_grp = jnp.max(gp, axis=-1, keepdims=True)
    grp = jnp.min(jnp.where(jnp.logical_and(in_g, gp == p_grp), lane, big), axis=-1, keepdims=True)
    lo = N_GROUPS + EPG * grp
    in_e = jnp.logical_and(lane >= lo, lane < lo + EPG)
    el = jnp.where(in_e, logits, -jnp.inf)
    ee = jnp.where(in_e, jnp.exp(el - jnp.max(el, axis=-1, keepdims=True)), 0.0)
    ep = jnp.where(in_e, ee / jnp.sum(ee, axis=-1, keepdims=True), -1.0)
    p1 = jnp.max(ep, axis=-1, keepdims=True)
    i1 = jnp.min(jnp.where(ep == p1, lane, big), axis=-1, keepdims=True)
    ep2 = jnp.where(lane == i1, -1.0, ep)
    p2 = jnp.max(ep2, axis=-1, keepdims=True)
    i2 = jnp.min(jnp.where(ep2 == p2, lane, big), axis=-1, keepdims=True)
    den = p1 + p2
    ids = jnp.where(lane == 0, i1 - N_GROUPS, jnp.where(lane == 1, i2 - N_GROUPS, 0))
    wts = jnp.where(lane == 0, p_grp * (p1 / den), jnp.where(lane == 1, p_grp * (p2 / den), 0.0))
    return ids, wts


def _merge_body(x_ref, yf_ref, yb_ref, bonus_ref, g_ref, o_ref, gt_ref, g1_ref, sc2_ref, sh2_ref,
                gnw_ref, gnb_ref, wor_ref, wom_ref, wout_ref, n2_ref, wrt_ref, brt_ref,
                x1_ref, h2_ref, rid_ref, rw_ref):
    parts = []
    for j in range(N_PAIR):
        sl = slice(j * LANE, (j + 1) * LANE)
        y = yf_ref[0, :, sl] + yb_ref[0, :, sl]
        yc = y - _head_sum(y) * (1.0 / HD)
        var = _head_sum(yc * yc) * (1.0 / HD)
        yn = yc * lax.rsqrt(var + GN_EPS) * gnw_ref[:, sl] + gnb_ref[:, sl]
        parts.append(((yn + bonus_ref[:, sl]) * g_ref[:, sl]).astype(BF16))
    y_a = _dot(jnp.concatenate(parts, axis=1), wor_ref[...])
    y_b = _dot(o_ref[...], wom_ref[...])
    mix = gt_ref[:, :D] * y_a + gt_ref[:, D:] * y_b
    x1 = x_ref[...] + g1_ref[0] * _dot(mix, wout_ref[...])
    x1_ref[...] = x1
    h2 = _rms(x1, n2_ref[...]) * (1.0 + sc2_ref[0]) + sh2_ref[0]
    h2_ref[...] = h2
    logits = jnp.dot(h2, wrt_ref[...], precision=lax.Precision.HIGHEST,
                     preferred_element_type=F32) + brt_ref[...]
    ids, wts = _route(logits)
    rid_ref[...] = ids
    rw_ref[...] = wts


def _merge(x2, y_f, y_b, bonus, g, o, gates, gate1, scale2, shift2, gn_w, gn_b, w_o_rwkv, w_o_mla,
           w_out, norm2, w_rt, b_rt, seq):
    n = x2.shape[0]
    tm = min(TOKEN_TILE, seq)
    bmap = lambda i: ((i * tm) // seq, 0, 0)
    full = lambda i: (0, 0)
    tokc = pl.BlockSpec((tm, C), lambda i: (i, 0))
    tokd = pl.BlockSpec((tm, D), lambda i: (i, 0))
    ydir = pl.BlockSpec((1, tm, C), lambda i: (0, i, 0))
    modv = pl.BlockSpec((1, 1, D), bmap)
    rt = pl.BlockSpec((tm, LANE), lambda i: (i, 0))
    return pl.pallas_call(
        _merge_body, grid=(n // tm,),
        in_specs=[tokd, ydir, ydir, tokc, tokc, tokd, pl.BlockSpec((tm, 2 * D), lambda i: (i, 0)),
                  modv, modv, modv,
                  pl.BlockSpec((1, C), full), pl.BlockSpec((1, C), full),
                  pl.BlockSpec((C, D), full), pl.BlockSpec((D, D), full), pl.BlockSpec((D, D), full),
                  pl.BlockSpec((1, D), full), pl.BlockSpec((D, LANE), full),
                  pl.BlockSpec((1, LANE), full)],
        out_specs=[tokd, tokd, rt, rt],
        out_shape=[jax.ShapeDtypeStruct((n, D), F32), jax.ShapeDtypeStruct((n, D), F32),
                   jax.ShapeDtypeStruct((n, LANE), jnp.int32),
                   jax.ShapeDtypeStruct((n, LANE), F32)],
        compiler_params=_cp(("parallel",)), name="merge")(
            x2, y_f, y_b, bonus, g, o, gates, gate1, scale2, shift2, gn_w, gn_b, w_o_rwkv,
            w_o_mla, w_out, norm2, w_rt, b_rt)


def _row_copy(src_hbm, dst_ref, sem, src_row, dst_row):
    return pltpu.make_async_copy(src_hbm.at[pl.ds(src_row, 1)], dst_ref.at[pl.ds(dst_row, 1)], sem)


def _gather_rows(idx_ref, offset, src_hbm, dst_ref, sem):
    rows = dst_ref.shape[0]

    def start(r, carry):
        _row_copy(src_hbm, dst_ref, sem, idx_ref[0, 0, offset + r], r).start()
        return carry

    def wait(r, carry):
        _row_copy(src_hbm, dst_ref, sem, 0, r).wait()
        return carry

    lax.fori_loop(0, rows, start, 0)
    lax.fori_loop(0, rows, wait, 0)


def _gather_body(idx_ref, src_hbm, out_ref, sem):
    _gather_rows(idx_ref, 0, src_hbm, out_ref, sem)


def _gather(idx, src):
    nblk = idx.shape[0]
    return pl.pallas_call(
        _gather_body, grid=(nblk,),
        in_specs=[pl.BlockSpec((1, 1, MOE_ROWS), lambda i: (i, 0, 0), memory_space=pltpu.SMEM),
                  pl.BlockSpec(memory_space=pl.ANY)],
        out_specs=pl.BlockSpec((MOE_ROWS, D), lambda i: (i, 0)),
        out_shape=jax.ShapeDtypeStruct((nblk * MOE_ROWS, D), src.dtype),
        scratch_shapes=[pltpu.SemaphoreType.DMA(())],
        compiler_params=_cp(("arbitrary",)), name="moe_gather")(idx, src)


def _ffn_body(be_ref, x_ref, w13_ref, w2_ref, y_ref):
    del be_ref
    hcat = _dot(x_ref[...], w13_ref[0])
    gt = hcat[:, :D_EXPERT]
    up = hcat[:, D_EXPERT:]
    y_ref[...] = _dot(gt * _sigmoid(gt) * up, w2_ref[0])


def _moe_ffn(blk_e, xg, w13, w2):
    p = xg.shape[0]
    grid_spec = pltpu.PrefetchScalarGridSpec(
        num_scalar_prefetch=1, grid=(p // MOE_ROWS,),
        in_specs=[pl.BlockSpec((MOE_ROWS, D), lambda i, be: (i, 0)),
                  pl.BlockSpec((1, D, 2 * D_EXPERT), lambda i, be: (be[i], 0, 0)),
                  pl.BlockSpec((1, D_EXPERT, D), lambda i, be: (be[i], 0, 0))],
        out_specs=pl.BlockSpec((MOE_ROWS, D), lambda i, be: (i, 0)))
    return pl.pallas_call(
        _ffn_body, grid_spec=grid_spec, out_shape=jax.ShapeDtypeStruct((p, D), F32),
        compiler_params=_cp(("arbitrary",)), name="moe_ffn")(blk_e, xg, w13, w2)


def _combine_body(pos_ref, yb_hbm, x1_ref, rw_ref, g2_ref, out_ref, ya_ref, yc_ref, sem):
    tm = x1_ref.shape[0]
    _gather_rows(pos_ref, 0, yb_hbm, ya_ref, sem)
    _gather_rows(pos_ref, tm, yb_hbm, yc_ref, sem)
    moe = rw_ref[:, 0:1] * ya_ref[...] + rw_ref[:, 1:2] * yc_ref[...]
    out_ref[...] = x1_ref[...] + g2_ref[0] * moe


def _combine(pos, yb, x1, rw, gate2, seq):
    n = x1.shape[0]
    tm = min(TOKEN_TILE, seq)
    return pl.pallas_call(
        _combine_body, grid=(n // tm,),
        in_specs=[pl.BlockSpec((1, 1, 2 * tm), lambda i: (i, 0, 0), memory_space=pltpu.SMEM),
                  pl.BlockSpec(memory_space=pl.ANY),
                  pl.BlockSpec((tm, D), lambda i: (i, 0)),
                  pl.BlockSpec((tm, LANE), lambda i: (i, 0)),
                  pl.BlockSpec((1, 1, D), lambda i: ((i * tm) // seq, 0, 0))],
        out_specs=pl.BlockSpec((tm, D), lambda i: (i, 0)),
        out_shape=jax.ShapeDtypeStruct((n, D), F32),
        scratch_shapes=[pltpu.VMEM((tm, D), F32), pltpu.VMEM((tm, D), F32),
                        pltpu.SemaphoreType.DMA(())],
        compiler_params=_cp(("arbitrary",)), name="moe_combine")(pos, yb, x1, rw, gate2)


def _moe_plan(rid, n):
    expert = rid[:, :2].reshape(-1)
    onehot = (expert[:, None] == jnp.arange(N_EXPERTS, dtype=jnp.int32)[None, :]).astype(jnp.int32)
    csum = jnp.cumsum(onehot, axis=0)
    rank = jnp.take_along_axis(csum, expert[:, None], axis=1)[:, 0] - 1
    counts = csum[-1]
    padded = (counts + MOE_ROWS - 1) // MOE_ROWS * MOE_ROWS
    pad_end = jnp.cumsum(padded)
    pad_start = pad_end - padded
    pos = pad_start[expert] + rank
    total = 2 * n + N_EXPERTS * MOE_ROWS
    nblk = total // MOE_ROWS
    token = jnp.repeat(jnp.arange(n, dtype=jnp.int32), 2)
    buf_tok = jnp.zeros((total,), jnp.int32).at[pos].set(token)
    blk_e = jnp.minimum(jnp.searchsorted(pad_end, jnp.arange(nblk, dtype=jnp.int32) * MOE_ROWS,
                                         side='right'), N_EXPERTS - 1).astype(jnp.int32)
    return pos.astype(jnp.int32), buf_tok, blk_e


def _rope_tables(seq):
    half = QK_ROPE // 2
    inv = 1.0 / (ROPE_THETA ** (jnp.arange(0, QK_ROPE, 2, dtype=F32) / QK_ROPE))
    ang = jnp.arange(seq, dtype=F32)[:, None] * inv[None, :]
    cos, sin = jnp.cos(ang), jnp.sin(ang)
    ones = jnp.ones((seq, QK_NOPE), F32)
    z16 = jnp.zeros((seq, half), F32)
    ztail = jnp.zeros((seq, LANE - QK_NOPE - QK_ROPE), F32)
    znope = jnp.zeros((seq, QK_NOPE), F32)
    cos_t = jnp.concatenate([ones, cos, cos, ztail], axis=1)
    sin_a = jnp.concatenate([znope, -sin, z16, ztail], axis=1)
    sin_b = jnp.concatenate([znope, z16, sin, ztail], axis=1)
    return cos_t, sin_a, sin_b


def _pad_heads(w, per_head):
    k = w.shape[0]
    w = w.reshape(k, MLA_H, per_head)
    return jnp.pad(w, ((0, 0), (0, 0), (0, LANE - per_head))).reshape(k, MLA_H * LANE)


def _prepare_weights(w_in, shift_mu, decay_up, iclr_up, gate_up, k_k, k_a, r_k, gn_w, gn_b, w_o_rwkv,
                     q_a_norm, w_uq, kv_a_norm, w_uk, w_uv, q_nope_norm, q_rope_norm, k_nope_norm,
                     k_rope_norm, w_o_mla, b_merge, w_out, norm1, norm2, w_rg, b_rg, w_re, b_re,
                     w13, w2):
    w = {}
    w_in = w_in[0]
    w['w_r'] = w_in[:, :RWKV_COLS].astype(BF16)
    mla_cols = Q_LORA + KV_LORA + QK_ROPE
    w['w_m'] = jnp.pad(w_in[:, RWKV_COLS:RWKV_COLS + mla_cols],
                       ((0, 0), (0, MLA_PAD - mla_cols))).astype(BF16)
    w['w_g'] = w_in[:, RWKV_COLS + mla_cols:].astype(BF16)
    w['b_merge'] = b_merge
    w['norm1'] = norm1
    w['norm2'] = norm2
    w['shift_mu'] = shift_mu[0]
    zero = jnp.zeros((HD, C), F32)
    w['decay_up2'] = jnp.concatenate(
        [jnp.concatenate([decay_up[0, 0], zero], axis=1),
         jnp.concatenate([zero, decay_up[0, 1]], axis=1)], axis=0).astype(BF16)
    w['iclr_up2'] = jnp.concatenate(
        [jnp.concatenate([iclr_up[0, 0], zero], axis=1),
         jnp.concatenate([zero, iclr_up[0, 1]], axis=1)], axis=0).astype(BF16)
    w['gate_up'] = gate_up[0].astype(BF16)
    w['k_k'], w['k_a'], w['gn_w'], w['gn_b'] = k_k, k_a, gn_w, gn_b
    w['r_k'] = r_k.reshape(1, C)
    w['w_o_rwkv'] = w_o_rwkv[0].astype(BF16)
    w['q_a_norm'], w['kv_a_norm'] = q_a_norm, kv_a_norm
    w['w_uq'] = _pad_heads(w_uq[0], QK_NOPE + QK_ROPE).astype(BF16)
    w['w_uk'] = _pad_heads(w_uk[0], QK_NOPE).astype(BF16)
    w['w_uv'] = _pad_heads(w_uv[0], HD).astype(BF16)
    e = jnp.zeros((LANE, MLA_H, LANE), F32)
    ar = jnp.arange(QK_ROPE)
    e = e.at[ar, :, QK_NOPE + ar].set(1.0)
    w['e_mat'] = e.reshape(LANE, MLA_H * LANE).astype(BF16)
    ztail = jnp.zeros((1, LANE - QK_NOPE - QK_ROPE), F32)
    w['q_gain'] = jnp.concatenate([q_nope_norm, q_rope_norm, ztail], axis=1)
    w['k_gain'] = jnp.concatenate([k_nope_norm, k_rope_norm, ztail], axis=1)
    wom = w_o_mla[0].reshape(MLA_H, HD, D)
    w['w_o_mla'] = jnp.pad(wom, ((0, 0), (0, LANE - HD), (0, 0))).reshape(MLA_H * LANE, D).astype(BF16)
    w['w_out'] = w_out[0].astype(BF16)
    n_rt = N_GROUPS + N_EXPERTS
    w['w_rt'] = jnp.pad(jnp.concatenate([w_rg[0], w_re[0]], axis=1), ((0, 0), (0, LANE - n_rt)))
    w['b_rt'] = jnp.pad(jnp.concatenate([b_rg, b_re], axis=1), ((0, 0), (0, LANE - n_rt)))
    w['w13'] = w13[0].astype(BF16)
    w['w2'] = w2[0].astype(BF16)
    return w


def _trunk(x, mod, w, decay_w0, iclr_a0):
    b, s, _ = x.shape
    n = b * s
    x2 = x.reshape(n, D)
    shift1, scale1, gate1, shift2, scale2, gate2 = [
        mod[:, i * D:(i + 1) * D].reshape(b, 1, D) for i in range(6)]

    p_rwkv, p_mla, gates = _inproj(x2, scale1, shift1, w['norm1'], w['w_r'], w['w_m'], w['w_g'],
                                   w['b_merge'], s)

    cos_t, sin_a, sin_b = _rope_tables(s)
    q, k, v = _mla_prep(p_mla, w['q_a_norm'], w['w_uq'], w['kv_a_norm'], w['w_uk'], w['w_uv'],
                        w['e_mat'], w['q_gain'], w['k_gain'], cos_t, sin_a, sin_b, s)
    hp = MLA_H * LANE
    o = _attention(q.reshape(b, s, hp), k.reshape(b, s, hp), v.reshape(b, s, hp)).reshape(n, hp)

    r, vv, kk, bonus, g, kd, lw, bb = _rwkv_prep(
        p_rwkv, w['shift_mu'], decay_w0, w['decay_up2'], iclr_a0, w['iclr_up2'], w['gate_up'],
        w['k_k'], w['k_a'], w['r_k'], s)
    y_f, y_b = _rwkv_chunk(r.reshape(b, s, C), vv.reshape(b, s, C), kk.reshape(b, s, C),
                           kd.reshape(2, b, s, C), lw.reshape(2, b, s, C), bb.reshape(2, b, s, C))

    x1, h2, rid, rw = _merge(x2, y_f.reshape(1, n, C), y_b.reshape(1, n, C), bonus, g, o, gates,
                             gate1, scale2, shift2, w['gn_w'], w['gn_b'], w['w_o_rwkv'],
                             w['w_o_mla'], w['w_out'], w['norm2'], w['w_rt'], w['b_rt'], s)

    pos, buf_tok, blk_e = _moe_plan(rid, n)
    xg = _gather(buf_tok.reshape(-1, 1, MOE_ROWS), h2)
    yb = _moe_ffn(blk_e, xg, w['w13'], w['w2'])
    tm = min(TOKEN_TILE, s)
    pos2 = pos.reshape(n // tm, tm, 2).transpose(0, 2, 1).reshape(n // tm, 1, 2 * tm)
    out = _combine(pos2, yb, x1, rw, gate2, s)
    return out.reshape(b, s, D)


def kernel(x_prompt, x_sample, c_prompt, c_sample, w_ada, b_ada, norm1, w_in, shift_mu, decay_w0, decay_up, iclr_a0, iclr_up, gate_up, k_k, k_a, r_k, gn_w, gn_b, w_o_rwkv, q_a_norm, w_uq, kv_a_norm, w_uk, w_uv, q_nope_norm, q_rope_norm, k_nope_norm, k_rope_norm, w_o_mla, b_merge, w_out, norm2, w_rg, b_rg, w_re, b_re, w13, w2):
    w = _prepare_weights(w_in, shift_mu, decay_up, iclr_up, gate_up, k_k, k_a, r_k, gn_w, gn_b,
                         w_o_rwkv, q_a_norm, w_uq, kv_a_norm, w_uk, w_uv, q_nope_norm, q_rope_norm,
                         k_nope_norm, k_rope_norm, w_o_mla, b_merge, w_out, norm1, norm2, w_rg, b_rg,
                         w_re, b_re, w13, w2)
    bp, bs = c_prompt.shape[0], c_sample.shape[0]
    rows = -(-(bp + bs) // 8) * 8
    c_all = jnp.pad(jnp.concatenate([c_prompt, c_sample], axis=0), ((0, rows - bp - bs), (0, 0)))
    mod = _ada(c_all, w_ada[0].astype(BF16), b_ada)
    y_prompt = _trunk(x_prompt, mod[:bp], w, decay_w0[0], iclr_a0[0])
    y_sample = _trunk(x_sample, mod[bp:bp + bs], w, decay_w0[0], iclr_a0[0])
    return (y_prompt, y_sample)
```
